```python
import jax, jax.numpy as jnp
from jax import lax
import numpy as np

D_MODEL = 1024
BATCH = 8
SEQ = 2048
DEPTH = 4

HEAD_DIM = 64
ATTN_WIDTH = D_MODEL
N_Q_HEADS = ATTN_WIDTH // HEAD_DIM
N_KV_HEADS = N_Q_HEADS // 4
Q_PER_KV = N_Q_HEADS // N_KV_HEADS
KV_WIDTH = N_KV_HEADS * HEAD_DIM
WINDOW = 128
ATTN_BLOCK = 128

D_INNER = D_MODEL
SSM_HEAD_DIM = 64
N_SSM_HEADS = D_INNER // SSM_HEAD_DIM
N_SSM_GROUPS = 2
SSM_HEADS_PER_GROUP = N_SSM_HEADS // N_SSM_GROUPS
D_STATE = 128
CONV_WIDTH = 4
CHUNK = 128
CONV_CH = D_INNER + 2 * N_SSM_GROUPS * D_STATE

MIX_WIDTH = ATTN_WIDTH + D_INNER
IN_SPLITS = tuple(int(s) for s in np.cumsum([ATTN_WIDTH, KV_WIDTH, KV_WIDTH, ATTN_WIDTH, D_INNER, CONV_CH]))
IN_WIDTH = IN_SPLITS[-1] + N_SSM_HEADS

DEEPNORM_ALPHA = (2.0 * DEPTH) ** 0.25
DEEPNORM_BETA = (8.0 * DEPTH) ** -0.25
LN_EPS = 1e-5
RMS_EPS = 1e-5

kernel_name = "hybrid_swa_sink_ssd_deepnorm"


def alibi_slopes():
    h = jnp.arange(1, N_Q_HEADS + 1, dtype=jnp.float32)
    return jnp.exp2(-8.0 * h / N_Q_HEADS)


def sliding_window_attention(q, k, v, sinks):
    b, l, _ = q.shape
    nb = l // ATTN_BLOCK
    q = q.reshape(b, nb, ATTN_BLOCK, N_KV_HEADS, Q_PER_KV, HEAD_DIM)

    def band(t):
        t = t.reshape(b, l, N_KV_HEADS, HEAD_DIM)
        t = jnp.pad(t, ((0, 0), (ATTN_BLOCK, 0), (0, 0), (0, 0)))
        t = t.reshape(b, nb + 1, ATTN_BLOCK, N_KV_HEADS, HEAD_DIM)
        return jnp.concatenate([t[:, :-1], t[:, 1:]], axis=2)

    kb, vb = band(k), band(v)
    scores = jnp.einsum("bnqkgd,bnskd->bnkgqs", q, kb).astype(jnp.float32) * (HEAD_DIM ** -0.5)
    qi = jnp.arange(ATTN_BLOCK)[:, None]
    sj = jnp.arange(2 * ATTN_BLOCK)[None, :]
    dist = qi - sj + ATTN_BLOCK
    blk = jnp.arange(nb)[:, None, None]
    valid = (dist >= 0) & (dist < WINDOW) & (blk * ATTN_BLOCK + sj - ATTN_BLOCK >= 0)
    slopes = alibi_slopes().reshape(N_KV_HEADS, Q_PER_KV)[:, :, None, None]
    scores = scores - slopes * dist.astype(jnp.float32)
    scores = jnp.where(valid[None, :, None, None], scores, -jnp.inf)
    sink = sinks.astype(jnp.float32).reshape(N_KV_HEADS, Q_PER_KV)[:, :, None, None]
    m = jnp.maximum(scores.max(axis=-1, keepdims=True), sink)
    p = jnp.exp(scores - m)
    probs = (p / (p.sum(axis=-1, keepdims=True) + jnp.exp(sink - m))).astype(v.dtype)
    out = jnp.einsum("bnkgqs,bnskd->bnqkgd", probs, vb)
    return out.reshape(b, l, ATTN_WIDTH)


def causal_depthwise_conv(u, w, bias):
    out = lax.conv_general_dilated(u, w[:, None, :], window_strides=(1,),
                                   padding=[(CONV_WIDTH - 1, 0)],
                                   dimension_numbers=("NWC", "WIO", "NWC"),
                                   feature_group_count=u.shape[-1])
    return out + bias


def exp_segsum(a_cs):
    n = a_cs.shape[-1]
    diff = a_cs[..., :, None] - a_cs[..., None, :]
    mask = jnp.tril(jnp.ones((n, n), dtype=bool))
    return jnp.exp(jnp.where(mask, diff, -jnp.inf))


def ssd_chunked(xs, dt, a, bm, cm):
    b, l, _ = xs.shape
    nc = l // CHUNK
    G, HG, P, N = N_SSM_GROUPS, SSM_HEADS_PER_GROUP, SSM_HEAD_DIM, D_STATE
    xc = xs.reshape(b, nc, CHUNK, G, HG, P)
    dtc = dt.reshape(b, nc, CHUNK, G, HG)
    bc = bm.reshape(b, nc, CHUNK, G, N)
    cc = cm.reshape(b, nc, CHUNK, G, N)
    a_cs = jnp.moveaxis(jnp.cumsum(dtc * a.reshape(G, HG), axis=2), 2, -1)
    xdt = xc * dtc[..., None]
    cb = jnp.einsum("bclgn,bcsgn->bcgls", cc, bc)
    y_diag = jnp.einsum("bcgls,bcghls,bcsghp->bclghp", cb, exp_segsum(a_cs), xdt)
    decay_to_end = jnp.exp(a_cs[..., -1:] - a_cs)
    states = jnp.einsum("bclgn,bcghl,bclghp->bcghpn", bc, decay_to_end, xdt)
    chunk_decay = jnp.exp(a_cs[..., -1])

    def step(carry, inp):
        st, dec = inp
        return carry * dec[..., None, None] + st, carry

    init = jnp.zeros(states.shape[:1] + states.shape[2:], states.dtype)
    _, prev = lax.scan(step, init, (jnp.moveaxis(states, 1, 0), jnp.moveaxis(chunk_decay, 1, 0)))
    prev = jnp.moveaxis(prev, 0, 1)
    y_off = jnp.einsum("bclgn,bcghpn,bcghl->bclghp", cc, prev, jnp.exp(a_cs))
    return (y_diag + y_off).astype(xs.dtype).reshape(b, l, D_INNER)


def gated_rmsnorm(y, z, w):
    b, l, _ = y.shape
    g = (y * jax.nn.silu(z)).astype(jnp.float32).reshape(b, l, N_SSM_GROUPS, D_INNER // N_SSM_GROUPS)
    g = g * lax.rsqrt(jnp.mean(g * g, axis=-1, keepdims=True) + RMS_EPS)
    return (g.reshape(b, l, D_INNER) * w.astype(jnp.float32)).astype(y.dtype)


def layer_norm(x, g, bias):
    xf = x.astype(jnp.float32)
    mu = jnp.mean(xf, axis=-1, keepdims=True)
    var = jnp.mean(jnp.square(xf - mu), axis=-1, keepdims=True)
    return ((xf - mu) * lax.rsqrt(var + LN_EPS) * g + bias).astype(x.dtype)


def hybrid_layer(x, w_in, conv_w, conv_b, dt_bias, a_log, d_skip, ssm_norm_w, sinks, w_out, ln_g, ln_b):
    b, l, _ = x.shape
    proj = jnp.einsum("bld,de->ble", x, w_in)
    q, k, v, z_attn, z_ssm, xbc, dt_raw = jnp.split(proj, IN_SPLITS, axis=-1)
    attn = sliding_window_attention(q, k, v, sinks) * jax.nn.silu(z_attn)
    xbc = jax.nn.silu(causal_depthwise_conv(xbc, conv_w, conv_b))
    xs, bm, cm = jnp.split(xbc, [D_INNER, D_INNER + N_SSM_GROUPS * D_STATE], axis=-1)
    dt = jax.nn.softplus(dt_raw.astype(jnp.float32) + dt_bias.astype(jnp.float32))
    a = -jnp.exp(a_log.astype(jnp.float32))
    y = ssd_chunked(xs, dt, a, bm, cm)
    y = y + (xs.reshape(b, l, N_SSM_HEADS, SSM_HEAD_DIM) * d_skip[:, None]).reshape(b, l, D_INNER)
    ssm = gated_rmsnorm(y, z_ssm, ssm_norm_w)
    out = jnp.einsum("ble,ed->bld", jnp.concatenate([attn, ssm], axis=-1), w_out)
    return layer_norm(DEEPNORM_ALPHA * x + out, ln_g, ln_b)


def setup_inputs(seed: int = 0) -> dict:
    key = jax.random.key(seed)
    ks = jax.random.split(key, 12)
    f32 = jnp.float32
    x = jax.random.normal(ks[0], (BATCH, SEQ, D_MODEL), f32)
    w_in = jax.random.normal(ks[1], (DEPTH, D_MODEL, IN_WIDTH), f32) * D_MODEL ** -0.5
    conv_w = jax.random.normal(ks[2], (DEPTH, CONV_WIDTH, CONV_CH), f32) * CONV_WIDTH ** -0.5
    conv_b = jax.random.normal(ks[3], (DEPTH, CONV_CH), f32) * 0.02
    dt0 = jnp.exp(jax.random.uniform(ks[4], (DEPTH, N_SSM_HEADS), f32) * (jnp.log(0.1) - jnp.log(0.001)) + jnp.log(0.001))
    dt_bias = dt0 + jnp.log(-jnp.expm1(-dt0))
    a_log = jnp.log(jax.random.uniform(ks[5], (DEPTH, N_SSM_HEADS), f32, 1.0, 16.0))
    d_skip = 1.0 + 0.1 * jax.random.normal(ks[6], (DEPTH, N_SSM_HEADS), f32)
    ssm_norm_w = 1.0 + 0.02 * jax.random.normal(ks[7], (DEPTH, D_INNER), f32)
    sinks = jax.random.normal(ks[8], (DEPTH, N_Q_HEADS), f32)
    w_out = jax.random.normal(ks[9], (DEPTH, MIX_WIDTH, D_MODEL), f32) * (MIX_WIDTH ** -0.5 * DEEPNORM_BETA)
    ln_g = 1.0 + 0.02 * jax.random.normal(ks[10], (DEPTH, D_MODEL), f32)
    ln_b = 0.02 * jax.random.normal(ks[11], (DEPTH, D_MODEL), f32)
    return {"x": x, "w_in": w_in, "conv_w": conv_w, "conv_b": conv_b, "dt_bias": dt_bias,
            "a_log": a_log, "d_skip": d_skip, "ssm_norm_w": ssm_norm_w, "sinks": sinks,
            "w_out": w_out, "ln_g": ln_g, "ln_b": ln_b}


def reference(x, w_in, conv_w, conv_b, dt_bias, a_log, d_skip, ssm_norm_w, sinks, w_out, ln_g, ln_b):
    h = x
    for i in range(DEPTH):
        h = hybrid_layer(h, w_in[i], conv_w[i], conv_b[i], dt_bias[i], a_log[i], d_skip[i],
                         ssm_norm_w[i], sinks[i], w_out[i], ln_g[i], ln_b[i])
    return h
```

```python
import functools

import jax
import jax.numpy as jnp
from jax import lax
from jax.experimental import pallas as pl
from jax.experimental.pallas import tpu as pltpu

F32 = jnp.float32
BF16 = jnp.bfloat16

D_MODEL = 1024
DEPTH = 4
HEAD_DIM = 64
N_Q_HEADS = 16
N_KV_HEADS = 4
KV_WIDTH = N_KV_HEADS * HEAD_DIM
BLK = 128
D_INNER = 1024
N_SSM_HEADS = 16
N_SSM_GROUPS = 2
GROUP_WIDTH = D_INNER // N_SSM_GROUPS
D_STATE = 128
CONV_WIDTH = 4
CONV_CH = D_INNER + 2 * N_SSM_GROUPS * D_STATE
MAIN_WIDTH = 5120
DEEPNORM_ALPHA = (2.0 * DEPTH) ** 0.25
LN_EPS = 1e-5
RMS_EPS = 1e-5

LANES = 128
CARRY_ROWS = 8
VMEM_LIMIT_BYTES = 56 * 1024 * 1024

PROJ_TILE = 256
OUT_TILE = 512


def _silu(x):
    return x * (1.0 / (1.0 + jnp.exp(-x)))


def _softplus(x):
    return jnp.maximum(x, 0.0) + jnp.log1p(jnp.exp(-jnp.abs(x)))


def _split_bf16(x, parts):
    out = []
    rem = x
    for _ in range(parts):
        t = rem.astype(BF16)
        out.append(t)
        rem = rem - t.astype(F32)
    return out


def _lane_lo_mask(shape):
    return lax.broadcasted_iota(jnp.int32, shape, len(shape) - 1) % LANES < HEAD_DIM


def _dot(a, b):
    return jnp.dot(a, b, preferred_element_type=F32)


def _dot_nt(a, b):
    return lax.dot_general(a, b, (((1,), (1,)), ((), ())), preferred_element_type=F32)


def _swap_halves(x):
    cols = [pltpu.roll(x[:, c:c + LANES], HEAD_DIM, 1) for c in range(0, x.shape[1], LANES)]
    return jnp.concatenate(cols, axis=1)


def _inproj_kernel(x_ref, w_ref, wdt_ref, convw_ref, convb_ref,
                   q_ref, k_ref, ksw_ref, v_ref, vsw_ref, za_ref, zs_ref,
                   xs_ref, b_ref, c_ref, dt_ref, ext_ref):
    tile = x_ref.shape[0]
    xb = x_ref[...].astype(BF16)

    def proj(lo, hi):
        return _dot(xb, w_ref[:, lo:hi])

    q_ref[...] = (proj(0, 1024) * (HEAD_DIM ** -0.5)).astype(BF16)
    k = proj(1024, 1280)
    k_ref[...] = k.astype(BF16)
    ksw_ref[...] = _swap_halves(k).astype(BF16)
    v = proj(1280, 1536)
    v_ref[...] = v.astype(BF16)
    vsw_ref[...] = _swap_halves(v).astype(BF16)
    za_ref[...] = proj(1536, 2560)
    zs_ref[...] = proj(2560, 3584)
    dt_ref[...] = _dot(xb, wdt_ref[...])

    @pl.when(pl.program_id(1) == 0)
    def _():
        ext_ref[0:CARRY_ROWS, :] = jnp.zeros((CARRY_ROWS, CONV_CH), F32)

    ext_ref[CARRY_ROWS:CARRY_ROWS + tile, :] = proj(3584, 5120)
    acc = convb_ref[...] + jnp.zeros((tile, CONV_CH), F32)
    for tap in range(CONV_WIDTH):
        start = CARRY_ROWS - (CONV_WIDTH - 1) + tap
        acc = acc + convw_ref[tap:tap + 1, :] * ext_ref[start:start + tile, :]
    ext_ref[0:CARRY_ROWS, :] = ext_ref[tile:tile + CARRY_ROWS, :]
    act = _silu(acc)
    xs_ref[...] = act[:, :D_INNER]
    b_ref[...] = act[:, D_INNER:D_INNER + N_SSM_GROUPS * D_STATE]
    c_ref[...] = act[:, D_INNER + N_SSM_GROUPS * D_STATE:]


def _inproj(x, w_main, w_dt, conv_w, conv_b):
    bsz, seq, _ = x.shape
    tile = PROJ_TILE
    grid = (bsz, seq // tile)

    def tok(width):
        return pl.BlockSpec((None, tile, width), lambda b, l: (b, l, 0))

    def full(shape):
        return pl.BlockSpec(shape, lambda b, l: (0,) * len(shape))

    widths = [(1024, BF16), (KV_WIDTH, BF16), (KV_WIDTH, BF16), (KV_WIDTH, BF16), (KV_WIDTH, BF16),
              (1024, F32), (1024, F32), (D_INNER, F32), (256, F32), (256, F32), (LANES, F32)]
    return pl.pallas_call(
        _inproj_kernel,
        grid=grid,
        in_specs=[tok(D_MODEL), full(w_main.shape), full(w_dt.shape), full(conv_w.shape), full(conv_b.shape)],
        out_specs=[tok(w) for w, _ in widths],
        out_shape=[jax.ShapeDtypeStruct((bsz, seq, w), dt) for w, dt in widths],
        scratch_shapes=[pltpu.VMEM((CARRY_ROWS + tile, CONV_CH), F32)],
        compiler_params=pltpu.CompilerParams(
            dimension_semantics=("arbitrary", "arbitrary"), vmem_limit_bytes=VMEM_LIMIT_BYTES),
        name="inproj_conv",
    )(x, w_main, w_dt, conv_w, conv_b)


def _attn_kernel(sinks_ref, q_ref, kc_ref, kp_ref, kswc_ref, kswp_ref, vc_ref, vp_ref, vswc_ref, vswp_ref,
                 za_ref, bias_ref, o_ref):
    first = (pl.program_id(1) == 0).astype(jnp.int32)
    lo_mask = _lane_lo_mask((2 * BLK, LANES))
    row_lo = lax.broadcasted_iota(jnp.int32, (2 * BLK, 1), 0) < BLK
    zero = jnp.zeros((2 * BLK, LANES), BF16)
    ones_lo = jnp.where(lo_mask, 1.0, 0.0).astype(BF16)
    ones_hi = jnp.where(lo_mask, 0.0, 1.0).astype(BF16)

    for g in range(N_KV_HEADS):
        col = pl.ds((g // 2) * LANES, LANES)

        def both(cur_ref, prev_ref):
            return jnp.concatenate([prev_ref[:, col], cur_ref[:, col]], axis=0)

        k_nat, k_sw = both(kc_ref, kp_ref), both(kswc_ref, kswp_ref)
        v_nat, v_sw = both(vc_ref, vp_ref), both(vswc_ref, vswp_ref)
        k_lo, k_hi = (k_nat, k_sw) if g % 2 == 0 else (k_sw, k_nat)
        v_lo, v_hi = (v_nat, v_sw) if g % 2 == 0 else (v_sw, v_nat)
        kpad = (jnp.where(lo_mask, k_lo, zero), jnp.where(lo_mask, zero, k_hi))
        vpad = (jnp.concatenate([jnp.where(lo_mask, v_lo, zero), ones_lo], axis=1),
                jnp.concatenate([jnp.where(lo_mask, zero, v_hi), ones_hi], axis=1))

        qs = jnp.concatenate([q_ref[:, pl.ds((2 * g) * LANES, LANES)],
                              q_ref[:, pl.ds((2 * g + 1) * LANES, LANES)]], axis=0)
        acc = None
        esink = []
        for e in range(2):
            s = _dot_nt(qs, kpad[e]) + bias_ref[first, 2 * g + e]
            sink = jnp.where(row_lo, sinks_ref[4 * g + e], sinks_ref[4 * g + 2 + e])
            m = jnp.maximum(jnp.max(s, axis=1, keepdims=True), sink)
            p = jnp.exp(s - m).astype(BF16)
            esink.append(jnp.exp(sink - m))
            part = _dot(p, vpad[e])
            acc = part if acc is None else acc + part
        den = acc[:, LANES:] + jnp.where(lo_mask, esink[0], esink[1])
        res = acc[:, :LANES] / den
        for j in range(2):
            c = pl.ds((2 * g + j) * LANES, LANES)
            o_ref[:, c] = (res[j * BLK:(j + 1) * BLK] * _silu(za_ref[:, c])).astype(BF16)


def _attention(sinks, q, k, ksw, v, vsw, za, bias):
    bsz, seq, _ = q.shape
    grid = (bsz, seq // BLK)

    def cur(width):
        return pl.BlockSpec((None, BLK, width), lambda b, n: (b, n, 0))

    def prev(width):
        return pl.BlockSpec((None, BLK, width), lambda b, n: (b, jnp.maximum(n - 1, 0), 0))

    kv = []
    for _ in range(4):
        kv += [cur(KV_WIDTH), prev(KV_WIDTH)]
    return pl.pallas_call(
        _attn_kernel,
        grid=grid,
        in_specs=[pl.BlockSpec(memory_space=pltpu.SMEM), cur(1024)] + kv
                 + [cur(1024), pl.BlockSpec(bias.shape, lambda b, n: (0, 0, 0, 0))],
        out_specs=cur(1024),
        out_shape=jax.ShapeDtypeStruct((bsz, seq, 1024), BF16),
        compiler_params=pltpu.CompilerParams(
            dimension_semantics=("arbitrary", "arbitrary"), vmem_limit_bytes=VMEM_LIMIT_BYTES),
        name="swa_attention",
    )(sinks, q, k, k, ksw, ksw, v, v, vsw, vsw, za, bias)


def _attention_bias():
    heads = jnp.arange(1, N_Q_HEADS + 1, dtype=F32)
    slopes = jnp.exp2(-8.0 * heads / N_Q_HEADS)
    qi = jnp.arange(BLK)[:, None]
    sj = jnp.arange(2 * BLK)[None, :]
    dist = qi - sj + BLK
    in_window = (dist >= 0) & (dist < BLK)
    per_head = -(slopes[:, None, None] * dist.astype(F32)[None])
    tables = []
    for first in (False, True):
        valid = in_window & (sj >= BLK) if first else in_window
        t = jnp.where(valid[None], per_head, -jnp.inf)
        rows = []
        for g in range(N_KV_HEADS):
            for e in range(2):
                rows.append(jnp.concatenate([t[4 * g + e], t[4 * g + 2 + e]], axis=0))
        tables.append(jnp.stack(rows))
    return jnp.stack([tables[0], tables[1]])


def _ssd_kernel(dtb_ref, alog_ref, dskip_ref, nw_ref, tri_ref, expand_ref,
                xs_ref, b_ref, c_ref, dt_ref, zs_ref, o_ref,
                state_ref, cst_ref, dtt_ref, y_ref):
    @pl.when(pl.program_id(1) == 0)
    def _():
        state_ref[...] = jnp.zeros_like(state_ref)

    dt = _softplus(dt_ref[...] + dtb_ref[...])
    da = dt * (-jnp.exp(alog_ref[...]))
    cs = _dot(tri_ref[...], jnp.concatenate(_split_bf16(da, 3), axis=0))
    cs_last = cs[BLK - 1:BLK, :]
    u = dt * jnp.exp(cs_last - cs)
    chunk_decay = jnp.exp(cs_last)
    exp_cs = jnp.exp(cs)
    cst_ref[...] = cs.T
    dtt_ref[...] = dt.T

    lhs = jnp.concatenate([u, jnp.broadcast_to(chunk_decay, (CARRY_ROWS, LANES))], axis=0)
    expanded = _dot(jnp.concatenate(_split_bf16(lhs, 2), axis=1), expand_ref[...])
    u_exp = expanded[:BLK]
    decay_exp = expanded[BLK:BLK + 1]

    xs = xs_ref[...]
    xsb = xs.astype(BF16)
    xw = (xs * u_exp).astype(BF16)
    tril = (lax.broadcasted_iota(jnp.int32, (BLK, BLK), 0)
            >= lax.broadcasted_iota(jnp.int32, (BLK, BLK), 1))
    lo_mask = _lane_lo_mask((BLK, LANES))

    for g in range(N_SSM_GROUPS):
        gcol = pl.ds(g * D_STATE, D_STATE)
        cg = c_ref[:, gcol]
        bt = b_ref[:, gcol].T.astype(BF16)
        cb = _dot(cg.astype(BF16), bt)
        new_state = _dot(bt, xw[:, g * GROUP_WIDTH:(g + 1) * GROUP_WIDTH])
        for pair in range(4):
            pcol = pl.ds((4 * g + pair) * LANES, LANES)
            lhs_rows = []
            for h in (2 * (4 * g + pair), 2 * (4 * g + pair) + 1):
                diff = cs[:, h:h + 1] - cst_ref[h:h + 1, :]
                decay = jnp.exp(jnp.where(tril, diff, -jnp.inf))
                intra = (cb * decay * dtt_ref[h:h + 1, :]).astype(BF16)
                inter = (cg * exp_cs[:, h:h + 1]).astype(BF16)
                lhs_rows.append(jnp.concatenate([intra, inter], axis=1))
            rhs = jnp.concatenate([xsb[:, (4 * g + pair) * LANES:(4 * g + pair + 1) * LANES],
                                   state_ref[:, pcol].astype(BF16)], axis=0)
            y2 = _dot(jnp.concatenate(lhs_rows, axis=0), rhs)
            y_ref[:, pcol] = jnp.where(lo_mask, y2[:BLK], y2[BLK:])
        scol = pl.ds(g * GROUP_WIDTH, GROUP_WIDTH)
        state_ref[:, scol] = (state_ref[:, scol] * decay_exp[:, g * GROUP_WIDTH:(g + 1) * GROUP_WIDTH]
                              + new_state)

    y = y_ref[...] + xs * dskip_ref[...]
    gated = y * _silu(zs_ref[...])
    for g in range(N_SSM_GROUPS):
        scol = pl.ds(g * GROUP_WIDTH, GROUP_WIDTH)
        gg = gated[:, g * GROUP_WIDTH:(g + 1) * GROUP_WIDTH]
        ms = jnp.mean(gg * gg, axis=1, keepdims=True)
        o_ref[:, scol] = (gg * lax.rsqrt(ms + RMS_EPS) * nw_ref[:, scol]).astype(BF16)


def _ssd(dt_bias, a_log, d_skip, norm_w, tri, expand, xs, bm, cm, dt_raw, zs):
    bsz, seq, _ = xs.shape
    grid = (bsz, seq // BLK)

    def cur(width):
        return pl.BlockSpec((None, BLK, width), lambda b, n: (b, n, 0))

    def full(shape):
        return pl.BlockSpec(shape, lambda b, n: (0,) * len(shape))

    params = [dt_bias, a_log, d_skip, norm_w, tri, expand]
    return pl.pallas_call(
        _ssd_kernel,
        grid=grid,
        in_specs=[full(p.shape) for p in params] + [cur(D_INNER), cur(256), cur(256), cur(LANES), cur(1024)],
        out_specs=cur(D_INNER),
        out_shape=jax.ShapeDtypeStruct((bsz, seq, D_INNER), BF16),
        scratch_shapes=[pltpu.VMEM((D_STATE, D_INNER), F32), pltpu.VMEM((LANES, BLK), F32),
                        pltpu.VMEM((LANES, BLK), F32), pltpu.VMEM((BLK, D_INNER), F32)],
        compiler_params=pltpu.CompilerParams(
            dimension_semantics=("arbitrary", "arbitrary"), vmem_limit_bytes=VMEM_LIMIT_BYTES),
        name="ssd_scan",
    )(*params, xs, bm, cm, dt_raw, zs)


def _out_kernel(attn_ref, ssm_ref, x_ref, w_ref, g_ref, b_ref, o_ref):
    acc = _dot(attn_ref[...], w_ref[0:1024, :]) + _dot(ssm_ref[...], w_ref[1024:2048, :])
    y = DEEPNORM_ALPHA * x_ref[...] + acc
    mu = jnp.mean(y, axis=1, keepdims=True)
    yc = y - mu
    var = jnp.mean(yc * yc, axis=1, keepdims=True)
    o_ref[...] = yc * lax.rsqrt(var + LN_EPS) * g_ref[...] + b_ref[...]


def _outproj(attn, ssm, x, w_out, ln_g, ln_b):
    bsz, seq, _ = x.shape
    tile = OUT_TILE
    grid = (bsz, seq // tile)

    def tok(width):
        return pl.BlockSpec((None, tile, width), lambda b, l: (b, l, 0))

    def full(shape):
        return pl.BlockSpec(shape, lambda b, l: (0,) * len(shape))

    return pl.pallas_call(
        _out_kernel,
        grid=grid,
        in_specs=[tok(1024), tok(1024), tok(D_MODEL), full(w_out.shape), full(ln_g.shape), full(ln_b.shape)],
        out_specs=tok(D_MODEL),
        out_shape=jax.ShapeDtypeStruct((bsz, seq, D_MODEL), F32),
        compiler_params=pltpu.CompilerParams(
            dimension_semantics=("arbitrary", "arbitrary"), vmem_limit_bytes=VMEM_LIMIT_BYTES),
        name="outproj_ln",
    )(attn, ssm, x, w_out, ln_g, ln_b)


def _pad_lanes(v):
    return jnp.pad(v, ((0, 0), (0, LANES - v.shape[1])))


def _layer(h, w_in, conv_w, conv_b, dt_bias, a_log, d_skip, ssm_norm_w, sinks, w_out, ln_g, ln_b,
           bias, tri, expand):
    w_main = w_in[:, :MAIN_WIDTH].astype(BF16)
    w_dt = _pad_lanes(w_in[:, MAIN_WIDTH:]).astype(BF16)
    q, k, ksw, v, vsw, za, zs, xs, bm, cm, dt_raw = _inproj(h, w_main, w_dt, conv_w, conv_b[None, :])
    attn = _attention(sinks, q, k, ksw, v, vsw, za, bias)
    ssm = _ssd(_pad_lanes(dt_bias[None, :]), _pad_lanes(a_log[None, :]),
               jnp.repeat(d_skip, HEAD_DIM)[None, :], ssm_norm_w[None, :], tri, expand,
               xs, bm, cm, dt_raw, zs)
    return _outproj(attn, ssm, h, w_out.astype(BF16), ln_g[None, :], ln_b[None, :])


def kernel(x, w_in, conv_w, conv_b, dt_bias, a_log, d_skip, ssm_norm_w, sinks, w_out, ln_g, ln_b):
    bias = _attention_bias()
    lower = (jnp.arange(BLK)[:, None] >= jnp.arange(BLK)[None, :]).astype(BF16)
    tri = jnp.concatenate([lower, lower, lower], axis=1)
    head_of_lane = jnp.arange(D_INNER)[None, :] // HEAD_DIM
    one_hot = (jnp.arange(LANES)[:, None] == head_of_lane).astype(BF16)
    expand = jnp.concatenate([one_hot, one_hot], axis=0)
    h = x
    for i in range(DEPTH):
        h = _layer(h, w_in[i], conv_w[i], conv_b[i], dt_bias[i], a_log[i], d_skip[i], ssm_norm_w[i],
                   sinks[i], w_out[i], ln_g[i], ln_b[i], bias, tri, expand)
    return h
```
